```python
import jax, jax.numpy as jnp
from jax import lax
import numpy as np

D_MODEL = 1024
BATCH = 4
SEQ = 8192
DEPTH = 2

MIX_WIDTH = (3 * D_MODEL) // 4
MEM_WIDTH = D_MODEL // 4
N_MEM = 256
MEM_HEADS = 4
MEM_HEAD_DIM = MEM_WIDTH // MEM_HEADS
CHUNK = 128
SG_GROUPS = 8
SG_GROUP_DIM = MIX_WIDTH // SG_GROUPS
RET_CHUNK = 128
RET_HEADS = 4
RET_V_DIM = MIX_WIDTH // RET_HEADS
RET_QK_DIM = RET_V_DIM // 2
ROPE_BASE = 10000.0
D_FF = 2816
N_EXPERTS = 8
TOP_K = 2
D_FF_EXPERT = 3584
MOE_BLOCK = 128
N_A = (DEPTH + 1) // 2
N_B = DEPTH // 2
DEEPNORM_ALPHA = (2.0 * DEPTH) ** 0.25
DEEPNORM_BETA = (8.0 * DEPTH) ** -0.25
LN_EPS = 1e-5

kernel_name = 'hybrid_sgmlp_retention_moe_encoder'


def layer_norm(x, g, b):
    xf = x.astype(jnp.float32)
    mu = jnp.mean(xf, -1, keepdims=True)
    var = jnp.mean(jnp.square(xf - mu), -1, keepdims=True)
    return ((xf - mu) * lax.rsqrt(var + LN_EPS)).astype(x.dtype) * g + b


def head_norm(o):
    mu = jnp.mean(o, -1, keepdims=True)
    var = jnp.mean(jnp.square(o - mu), -1, keepdims=True)
    on = (o - mu) * lax.rsqrt(var + LN_EPS)
    return on.reshape(o.shape[0], o.shape[1], -1)


def rope_tables(seq, dim):
    inv = ROPE_BASE ** (-jnp.arange(0, dim, 2, dtype=jnp.float32) / dim)
    ang = jnp.arange(seq, dtype=jnp.float32)[:, None] * inv[None, :]
    return jnp.cos(ang), jnp.sin(ang)


def apply_rope(t, cos, sin):
    half = t.shape[-1] // 2
    t1, t2 = t[..., :half], t[..., half:]
    c = cos[None, :, None, :]
    s = sin[None, :, None, :]
    return jnp.concatenate([t1 * c - t2 * s, t2 * c + t1 * s], axis=-1)


def memory_attention(mq, mem_k, mem_v):
    b, s, _ = mq.shape
    q = mq.reshape(b, s, MEM_HEADS, MEM_HEAD_DIM)
    scores = jnp.einsum('bshd,bmhd->bhsm', q, mem_k).astype(jnp.float32) * (MEM_HEAD_DIM ** -0.5)
    probs = jax.nn.softmax(scores, axis=-1).astype(mem_v.dtype)
    return jnp.einsum('bhsm,bmhd->bshd', probs, mem_v).reshape(b, s, MEM_WIDTH)


def spatial_gating(u, v, ln_g, ln_b, w_s, b_s):
    b, s, _ = v.shape
    v = layer_norm(v, ln_g, ln_b).reshape(b, s // CHUNK, CHUNK, SG_GROUPS, SG_GROUP_DIM)
    mixed = jnp.einsum('gpq,bnqgd->bnpgd', w_s, v) + b_s.T[:, :, None]
    return u * mixed.reshape(b, s, MIX_WIDTH)


def chunk_scan(qd, kd, vc, chunk_decay, reverse):
    b, _, _, h, dk = qd.shape
    dv = vc.shape[-1]

    def step(state, xs):
        qn, kn, vn = xs
        out = jnp.einsum('bihd,bhde->bihe', qn, state)
        state = state * chunk_decay[None, :, None, None] + jnp.einsum('bjhd,bjhe->bhde', kn, vn)
        return state, out

    xs = (jnp.moveaxis(qd, 1, 0), jnp.moveaxis(kd, 1, 0), jnp.moveaxis(vc, 1, 0))
    init = jnp.zeros((b, h, dk, dv), jnp.float32)
    _, out = lax.scan(step, init, xs, reverse=reverse)
    return jnp.moveaxis(out, 0, 1)


def bidirectional_retention(q, k, v, log_gf, log_gb):
    b, s, h, dk = q.shape
    dv = v.shape[-1]
    nc = s // RET_CHUNK
    pos = jnp.arange(RET_CHUNK, dtype=jnp.float32)
    qc = (q * dk ** -0.5).reshape(b, nc, RET_CHUNK, h, dk)
    kc = k.reshape(b, nc, RET_CHUNK, h, dk)
    vc = v.reshape(b, nc, RET_CHUNK, h, dv)
    diff = pos[:, None] - pos[None, :]
    expo = jnp.where(diff[None] >= 0, diff[None] * log_gf[:, None, None], -diff[None] * log_gb[:, None, None])
    decay_mask = jnp.exp(expo)
    scores = jnp.einsum('bnihd,bnjhd->bnhij', qc, kc) * decay_mask
    o_inner = jnp.einsum('bnhij,bnjhe->bnihe', scores, vc)
    q_f = qc * jnp.exp(pos[:, None] * log_gf)[:, :, None]
    k_f = kc * jnp.exp((RET_CHUNK - pos)[:, None] * log_gf)[:, :, None]
    o_f = chunk_scan(q_f, k_f, vc, jnp.exp(RET_CHUNK * log_gf), reverse=False)
    q_b = qc * jnp.exp((RET_CHUNK - 1 - pos)[:, None] * log_gb)[:, :, None]
    k_b = kc * jnp.exp((pos + 1)[:, None] * log_gb)[:, :, None]
    o_b = chunk_scan(q_b, k_b, vc, jnp.exp(RET_CHUNK * log_gb), reverse=True)
    return (o_inner + o_f + o_b).reshape(b, s, h, dv)


def dense_swiglu(x, wg, wu, wd):
    return (jax.nn.silu(x @ wg) * (x @ wu)) @ wd


def moe_swiglu(x, w_router, b_router, w_gate, w_up, w_down):
    b, s, d = x.shape
    xt = x.reshape(-1, d)
    n_tok = xt.shape[0]
    n_asg = n_tok * TOP_K
    logits = (xt @ w_router).astype(jnp.float32) + b_router.astype(jnp.float32)
    top_logit, top_idx = lax.top_k(logits, TOP_K)
    gate = jax.nn.softmax(top_logit, axis=-1)
    exp_id = top_idx.reshape(-1)
    tok_id = jnp.arange(n_asg) // TOP_K
    order = jnp.argsort(exp_id)
    sorted_exp = exp_id[order]
    counts = jnp.bincount(exp_id, length=N_EXPERTS)
    padded = (counts + MOE_BLOCK - 1) // MOE_BLOCK * MOE_BLOCK
    pad_end = jnp.cumsum(padded)
    pad_start = pad_end - padded
    grp_start = jnp.cumsum(counts) - counts
    dest = pad_start[sorted_exp] + (jnp.arange(n_asg) - grp_start[sorted_exp])
    n_rows = n_asg + N_EXPERTS * MOE_BLOCK
    n_blocks = n_rows // MOE_BLOCK
    sorted_tok = tok_id[order]
    buf = jnp.zeros((n_rows, d), x.dtype).at[dest].set(xt[sorted_tok])
    blk_exp = jnp.minimum(jnp.searchsorted(pad_end, jnp.arange(n_blocks) * MOE_BLOCK, side='right'), N_EXPERTS - 1)

    def expert_block(args):
        xb, e = args
        return (jax.nn.silu(xb @ w_gate[e]) * (xb @ w_up[e])) @ w_down[e]

    yb = lax.map(expert_block, (buf.reshape(n_blocks, MOE_BLOCK, d), blk_exp))
    y_rows = yb.reshape(n_rows, d)[dest]
    w = gate.reshape(-1)[order].astype(x.dtype)
    y = jax.ops.segment_sum(y_rows * w[:, None], sorted_tok, num_segments=n_tok)
    return y.reshape(b, s, d)


def setup_inputs(seed: int = 0) -> dict:
    key = jax.random.key(seed)
    ks = iter(jax.random.split(key, 32))
    f32 = jnp.float32

    def dense(shape, fan_in, scale=1.0):
        return jax.random.normal(next(ks), shape, f32) * (scale * fan_in ** -0.5)

    def gain(shape):
        return 1.0 + 0.02 * jax.random.normal(next(ks), shape, f32)

    def small(shape, scale=0.02):
        return scale * jax.random.normal(next(ks), shape, f32)

    gam = 1.0 - 2.0 ** (-5.0 - jnp.arange(RET_HEADS, dtype=f32))
    decay_base = jnp.log(gam) - jnp.log1p(-gam)
    n_in_b = 2 * RET_HEADS * RET_QK_DIM + 2 * MIX_WIDTH + MEM_WIDTH
    return {
        'x': jax.random.normal(next(ks), (BATCH, SEQ, D_MODEL), f32),
        'mem': jax.random.normal(next(ks), (BATCH, N_MEM, D_MODEL), f32),
        'w_mem_kv': dense((D_MODEL, 2 * MEM_WIDTH), D_MODEL),
        'w_in_a': dense((N_A, D_MODEL, 2 * MIX_WIDTH + MEM_WIDTH), D_MODEL),
        'sg_ln_g': gain((N_A, MIX_WIDTH)),
        'sg_ln_b': small((N_A, MIX_WIDTH)),
        'sg_w': dense((N_A, SG_GROUPS, CHUNK, CHUNK), CHUNK, 0.5),
        'sg_b': gain((N_A, SG_GROUPS, CHUNK)),
        'w_in_b': dense((N_B, D_MODEL, n_in_b), D_MODEL),
        'decay_logit_f': decay_base + small((N_B, RET_HEADS), 0.1),
        'decay_logit_b': decay_base + small((N_B, RET_HEADS), 0.1),
        'ret_gn_g': gain((N_B, MIX_WIDTH)),
        'ret_gn_b': small((N_B, MIX_WIDTH)),
        'w_out': dense((DEPTH, D_MODEL, D_MODEL), D_MODEL, DEEPNORM_BETA),
        'ln_g': gain((DEPTH, 2, D_MODEL)),
        'ln_b': small((DEPTH, 2, D_MODEL)),
        'w_ff_gate': dense((N_A, D_MODEL, D_FF), D_MODEL),
        'w_ff_up': dense((N_A, D_MODEL, D_FF), D_MODEL),
        'w_ff_down': dense((N_A, D_FF, D_MODEL), D_FF, DEEPNORM_BETA),
        'w_router': dense((N_B, D_MODEL, N_EXPERTS), D_MODEL),
        'b_router': small((N_B, N_EXPERTS), 0.01),
        'w_e_gate': dense((N_B, N_EXPERTS, D_MODEL, D_FF_EXPERT), D_MODEL),
        'w_e_up': dense((N_B, N_EXPERTS, D_MODEL, D_FF_EXPERT), D_MODEL),
        'w_e_down': dense((N_B, N_EXPERTS, D_FF_EXPERT, D_MODEL), D_FF_EXPERT, DEEPNORM_BETA),
    }


def reference(x, mem, w_mem_kv, w_in_a, sg_ln_g, sg_ln_b, sg_w, sg_b, w_in_b, decay_logit_f, decay_logit_b, ret_gn_g, ret_gn_b, w_out, ln_g, ln_b, w_ff_gate, w_ff_up, w_ff_down, w_router, b_router, w_e_gate, w_e_up, w_e_down):
    b, s, _ = x.shape
    f32 = jnp.float32
    mem_k, mem_v = jnp.split(mem @ w_mem_kv, 2, axis=-1)
    mem_k = mem_k.reshape(b, N_MEM, MEM_HEADS, MEM_HEAD_DIM)
    mem_v = mem_v.reshape(b, N_MEM, MEM_HEADS, MEM_HEAD_DIM)
    cos, sin = rope_tables(s, RET_QK_DIM)
    qk_w = RET_HEADS * RET_QK_DIM
    for i in range(DEPTH):
        j = i // 2
        if i % 2 == 0:
            h = x @ w_in_a[j]
            u, v, mq = jnp.split(h, [MIX_WIDTH, 2 * MIX_WIDTH], axis=-1)
            mix = spatial_gating(jax.nn.gelu(u, approximate=False), jax.nn.gelu(v, approximate=False),
                                 sg_ln_g[j], sg_ln_b[j], sg_w[j], sg_b[j])
        else:
            h = x @ w_in_b[j]
            q, k, v, g, mq = jnp.split(h, [qk_w, 2 * qk_w, 2 * qk_w + MIX_WIDTH, 2 * qk_w + 2 * MIX_WIDTH], axis=-1)
            q = apply_rope(q.reshape(b, s, RET_HEADS, RET_QK_DIM).astype(f32), cos, sin)
            k = apply_rope(k.reshape(b, s, RET_HEADS, RET_QK_DIM).astype(f32), cos, sin)
            v = v.reshape(b, s, RET_HEADS, RET_V_DIM).astype(f32)
            log_gf = jax.nn.log_sigmoid(decay_logit_f[j].astype(f32))
            log_gb = jax.nn.log_sigmoid(decay_logit_b[j].astype(f32))
            o = bidirectional_retention(q, k, v, log_gf, log_gb)
            o = head_norm(o).astype(x.dtype) * ret_gn_g[j] + ret_gn_b[j]
            mix = jax.nn.silu(g) * o
        y = jnp.concatenate([mix, memory_attention(mq, mem_k, mem_v)], axis=-1) @ w_out[i]
        x = layer_norm(DEEPNORM_ALPHA * x + y, ln_g[i, 0], ln_b[i, 0])
        if i % 2 == 0:
            f = dense_swiglu(x, w_ff_gate[j], w_ff_up[j], w_ff_down[j])
        else:
            f = moe_swiglu(x, w_router[j], b_router[j], w_e_gate[j], w_e_up[j], w_e_down[j])
        x = layer_norm(DEEPNORM_ALPHA * x + f, ln_g[i, 1], ln_b[i, 1])
    return x
```

```python
import functools

import numpy as np
import jax
import jax.numpy as jnp
from jax import lax
from jax.experimental import pallas as pl
from jax.experimental.pallas import tpu as pltpu

F32 = jnp.float32
BF16 = jnp.bfloat16

D_MODEL = 1024
MIX_WIDTH = 768
MEM_WIDTH = 256
N_MEM = 256
MEM_HEADS = 4
MEM_HEAD_DIM = 64
CHUNK = 128
SG_GROUPS = 8
SG_GROUP_DIM = 96
RET_HEADS = 4
RET_V_DIM = 192
RET_QK_DIM = 96
QK_WIDTH = RET_HEADS * RET_QK_DIM
ROPE_BASE = 10000.0
N_EXPERTS = 8
TOP_K = 2
DEPTH = 2
DEEPNORM_ALPHA = (2.0 * DEPTH) ** 0.25
LN_EPS = 1e-5

TOKEN_TILE = 512
FFN_ROW_TILE = 512
FFN_COL_TILE = 1408
MOE_ROW_TILE = 512
MOE_COL_TILE = 512
ROUTER_ROWS = 16
VMEM_LIMIT_BYTES = 56 * 1024 * 1024


def _params(n_axes):
    return pltpu.CompilerParams(
        dimension_semantics=("arbitrary",) * n_axes,
        vmem_limit_bytes=VMEM_LIMIT_BYTES,
    )


def _dot(a, b):
    return jnp.dot(a, b, preferred_element_type=F32)


def _dot_nt(a, b):
    return lax.dot_general(a, b, (((1,), (1,)), ((), ())), preferred_element_type=F32)


def _dot_tn(a, b):
    return lax.dot_general(a, b, (((0,), (0,)), ((), ())), preferred_element_type=F32)


def _layer_norm(z, g, b):
    mu = jnp.mean(z, axis=-1, keepdims=True)
    zc = z - mu
    var = jnp.mean(zc * zc, axis=-1, keepdims=True)
    return zc * lax.rsqrt(var + LN_EPS) * g + b


def _gelu(x):
    return 0.5 * x * (1.0 + lax.erf(x * np.float32(np.sqrt(0.5))))


def _silu(x):
    return x * jax.nn.sigmoid(x)


def _col_range_mask(width, lo, hi):
    col = lax.broadcasted_iota(jnp.int32, (1, width), 1)
    return (col >= lo) & (col < hi)


def _memory_attention(mq, kbd_ref, vbd_ref):
    s = _dot(mq.astype(BF16), kbd_ref[...])
    es, sums = [], []
    for h in range(MEM_HEADS):
        seg = s[:, h * N_MEM:(h + 1) * N_MEM]
        e = jnp.exp(seg - jnp.max(seg, axis=-1, keepdims=True))
        es.append(e.astype(BF16))
        sums.append(jnp.sum(e, axis=-1, keepdims=True))
    att = _dot(jnp.concatenate(es, axis=1), vbd_ref[...])
    denom = sums[MEM_HEADS - 1]
    for h in range(MEM_HEADS - 2, -1, -1):
        denom = jnp.where(_col_range_mask(MEM_WIDTH, 0, (h + 1) * MEM_HEAD_DIM), sums[h], denom)
    return att / denom


def _out_proj_norm(x, mix, att, wout_ref, g_ref, b_ref):
    cat = jnp.concatenate([mix.astype(BF16), att.astype(BF16)], axis=1)
    y = _dot(cat, wout_ref[...])
    return _layer_norm(DEEPNORM_ALPHA * x + y, g_ref[...], b_ref[...])


def _mem_kv_kernel(mem_ref, w_ref, kbd_ref, vbd_ref):
    kv = _dot(mem_ref[0].astype(BF16), w_ref[...])
    k = kv[:, :MEM_WIDTH] * (MEM_HEAD_DIM ** -0.5)
    v = kv[:, MEM_WIDTH:]
    kt = k.T
    row_h = lax.broadcasted_iota(jnp.int32, (MEM_WIDTH, MEM_HEADS * N_MEM), 0) // MEM_HEAD_DIM
    col_h = lax.broadcasted_iota(jnp.int32, (MEM_WIDTH, MEM_HEADS * N_MEM), 1) // N_MEM
    kbd = jnp.where(row_h == col_h, jnp.concatenate([kt] * MEM_HEADS, axis=1), 0.0)
    kbd_ref[0] = kbd.astype(BF16)
    row_h = lax.broadcasted_iota(jnp.int32, (MEM_HEADS * N_MEM, MEM_WIDTH), 0) // N_MEM
    col_h = lax.broadcasted_iota(jnp.int32, (MEM_HEADS * N_MEM, MEM_WIDTH), 1) // MEM_HEAD_DIM
    vbd = jnp.where(row_h == col_h, jnp.concatenate([v] * MEM_HEADS, axis=0), 0.0)
    vbd_ref[0] = vbd.astype(BF16)


def _mem_kv(mem, w_mem_kv):
    b = mem.shape[0]
    return pl.pallas_call(
        _mem_kv_kernel,
        grid=(b,),
        in_specs=[
            pl.BlockSpec((1, N_MEM, D_MODEL), lambda i: (i, 0, 0)),
            pl.BlockSpec((D_MODEL, 2 * MEM_WIDTH), lambda i: (0, 0)),
        ],
        out_specs=[
            pl.BlockSpec((1, MEM_WIDTH, MEM_HEADS * N_MEM), lambda i: (i, 0, 0)),
            pl.BlockSpec((1, MEM_HEADS * N_MEM, MEM_WIDTH), lambda i: (i, 0, 0)),
        ],
        out_shape=[
            jax.ShapeDtypeStruct((b, MEM_WIDTH, MEM_HEADS * N_MEM), BF16),
            jax.ShapeDtypeStruct((b, MEM_HEADS * N_MEM, MEM_WIDTH), BF16),
        ],
        compiler_params=_params(1),
        name="mem_kv",
    )(mem, w_mem_kv.astype(BF16))


def _sg_layer_kernel(x_ref, win_ref, sglg_ref, sglb_ref, wcat_ref, sgbias_ref, kbd_ref, vbd_ref,
                     wout_ref, lng_ref, lnb_ref, o_ref):
    x = x_ref[...]
    h = _dot(x.astype(BF16), win_ref[...])
    u = _gelu(h[:, :MIX_WIDTH])
    v = _gelu(h[:, MIX_WIDTH:2 * MIX_WIDTH])
    mq = h[:, 2 * MIX_WIDTH:]
    vn = _layer_norm(v, sglg_ref[...], sglb_ref[...]).astype(BF16)
    group_masks = [_col_range_mask(MIX_WIDTH, g * SG_GROUP_DIM, (g + 1) * SG_GROUP_DIM)
                   for g in range(SG_GROUPS)]
    mixed = []
    for c in range(TOKEN_TILE // CHUNK):
        vc = vn[c * CHUNK:(c + 1) * CHUNK]
        vstack = jnp.concatenate([jnp.where(m, vc, jnp.zeros_like(vc)) for m in group_masks], axis=0)
        mixed.append(_dot(wcat_ref[...], vstack) + sgbias_ref[...])
    mix = u * jnp.concatenate(mixed, axis=0)
    att = _memory_attention(mq, kbd_ref.at[0], vbd_ref.at[0])
    o_ref[...] = _out_proj_norm(x, mix, att, wout_ref, lng_ref, lnb_ref)


def _sg_layer(x2d, tiles_per_batch, win, sglg, sglb, wcat, sgbias, kbd, vbd, wout, lng, lnb):
    n = x2d.shape[0]
    const = lambda i: (0, 0)
    per_batch = lambda i: (i // tiles_per_batch, 0, 0)
    return pl.pallas_call(
        _sg_layer_kernel,
        grid=(n // TOKEN_TILE,),
        in_specs=[
            pl.BlockSpec((TOKEN_TILE, D_MODEL), lambda i: (i, 0)),
            pl.BlockSpec(win.shape, const),
            pl.BlockSpec(sglg.shape, const),
            pl.BlockSpec(sglb.shape, const),
            pl.BlockSpec(wcat.shape, const),
            pl.BlockSpec(sgbias.shape, const),
            pl.BlockSpec((1,) + kbd.shape[1:], per_batch),
            pl.BlockSpec((1,) + vbd.shape[1:], per_batch),
            pl.BlockSpec(wout.shape, const),
            pl.BlockSpec(lng.shape, const),
            pl.BlockSpec(lnb.shape, const),
        ],
        out_specs=pl.BlockSpec((TOKEN_TILE, D_MODEL), lambda i: (i, 0)),
        out_shape=jax.ShapeDtypeStruct((n, D_MODEL), F32),
        compiler_params=_params(1),
        name="sg_layer",
    )(x2d, win, sglg, sglb, wcat, sgbias, kbd, vbd, wout, lng, lnb)


def _dense_ffn_kernel(x_ref, wg_ref, wu_ref, wd_ref, lng_ref, lnb_ref, o_ref, acc_ref):
    j = pl.program_id(1)

    @pl.when(j == 0)
    def _():
        acc_ref[...] = jnp.zeros_like(acc_ref)

    xb = x_ref[...].astype(BF16)
    hh = _silu(_dot(xb, wg_ref[...])) * _dot(xb, wu_ref[...])
    acc_ref[...] += _dot(hh.astype(BF16), wd_ref[...])

    @pl.when(j == pl.num_programs(1) - 1)
    def _():
        o_ref[...] = _layer_norm(DEEPNORM_ALPHA * x_ref[...] + acc_ref[...], lng_ref[...], lnb_ref[...])


def _dense_ffn(x2d, wg, wu, wd, lng, lnb):
    n = x2d.shape[0]
    d_ff = wg.shape[1]
    return pl.pallas_call(
        _dense_ffn_kernel,
        grid=(n // FFN_ROW_TILE, d_ff // FFN_COL_TILE),
        in_specs=[
            pl.BlockSpec((FFN_ROW_TILE, D_MODEL), lambda i, j: (i, 0)),
            pl.BlockSpec((D_MODEL, FFN_COL_TILE), lambda i, j: (0, j)),
            pl.BlockSpec((D_MODEL, FFN_COL_TILE), lambda i, j: (0, j)),
            pl.BlockSpec((FFN_COL_TILE, D_MODEL), lambda i, j: (j, 0)),
            pl.BlockSpec(lng.shape, lambda i, j: (0, 0)),
            pl.BlockSpec(lnb.shape, lambda i, j: (0, 0)),
        ],
        out_specs=pl.BlockSpec((FFN_ROW_TILE, D_MODEL), lambda i, j: (i, 0)),
        out_shape=jax.ShapeDtypeStruct((n, D_MODEL), F32),
        scratch_shapes=[pltpu.VMEM((FFN_ROW_TILE, D_MODEL), F32)],
        compiler_params=_params(2),
        name="dense_ffn",
    )(x2d, wg, wu, wd, lng, lnb)


def _ret_bwd_kernel(x_ref, wkv_ref, cos_ref, sin_ref, dkb_ref, rowdec_ref, bdmask_ref,
                    k_ref, v_ref, sb_ref, state_ref):
    @pl.when(pl.program_id(1) == 0)
    def _():
        state_ref[...] = jnp.zeros_like(state_ref)

    h = _dot(x_ref[...].astype(BF16), wkv_ref[...])
    k = h[:, :QK_WIDTH] * cos_ref[...] + h[:, QK_WIDTH:2 * QK_WIDTH] * sin_ref[...]
    vb = h[:, 2 * QK_WIDTH:].astype(BF16)
    k_ref[...] = k.astype(BF16)
    v_ref[...] = vb
    for c in range(TOKEN_TILE // CHUNK - 1, -1, -1):
        sb_ref[c] = state_ref[...].astype(BF16)
        kb = (k[c * CHUNK:(c + 1) * CHUNK] * dkb_ref[...]).astype(BF16)
        upd = _dot_tn(kb, vb[c * CHUNK:(c + 1) * CHUNK])
        state_ref[...] = state_ref[...] * rowdec_ref[...] + upd * bdmask_ref[...]


def _ret_bwd(x2d, batch, wkv, cos_k, sin_k, dkb, rowdec_b, bdmask):
    n = x2d.shape[0]
    nt = n // batch // TOKEN_TILE
    nc = TOKEN_TILE // CHUNK
    rev = lambda b, j: (b * nt + (nt - 1 - j), 0)
    const = lambda b, j: (0, 0)
    return pl.pallas_call(
        _ret_bwd_kernel,
        grid=(batch, nt),
        in_specs=[
            pl.BlockSpec((TOKEN_TILE, D_MODEL), rev),
            pl.BlockSpec(wkv.shape, const),
            pl.BlockSpec((TOKEN_TILE, QK_WIDTH), lambda b, j: (nt - 1 - j, 0)),
            pl.BlockSpec((TOKEN_TILE, QK_WIDTH), lambda b, j: (nt - 1 - j, 0)),
            pl.BlockSpec(dkb.shape, const),
            pl.BlockSpec(rowdec_b.shape, const),
            pl.BlockSpec(bdmask.shape, const),
        ],
        out_specs=[
            pl.BlockSpec((TOKEN_TILE, QK_WIDTH), rev),
            pl.BlockSpec((TOKEN_TILE, MIX_WIDTH), rev),
            pl.BlockSpec((nc, QK_WIDTH, MIX_WIDTH), lambda b, j: (b * nt + (nt - 1 - j), 0, 0)),
        ],
        out_shape=[
            jax.ShapeDtypeStruct((n, QK_WIDTH), BF16),
            jax.ShapeDtypeStruct((n, MIX_WIDTH), BF16),
            jax.ShapeDtypeStruct((n // CHUNK, QK_WIDTH, MIX_WIDTH), BF16),
        ],
        scratch_shapes=[pltpu.VMEM((QK_WIDTH, MIX_WIDTH), F32)],
        compiler_params=_params(2),
        name="ret_bwd",
    )(x2d, wkv, cos_k, sin_k, dkb, rowdec_b, bdmask)


def _ret_layer_kernel(x_ref, wqgm_ref, cos_ref, sin_ref, k_ref, v_ref, sb_ref,
                      dqf_ref, dkf_ref, dqb_ref, mcat_ref, rowdec_ref, bdmask_ref,
                      gng_ref, gnb_ref, kbd_ref, vbd_ref, wout_ref, lng_ref, lnb_ref,
                      wrh_ref, wrl_ref, br_ref, tri_ref,
                      o_ref, ridx_ref, rgate_ref, cnt_ref, state_ref, carry_ref):
    first = (pl.program_id(0) == 0) & (pl.program_id(1) == 0)

    @pl.when(pl.program_id(1) == 0)
    def _():
        state_ref[...] = jnp.zeros_like(state_ref)

    @pl.when(first)
    def _():
        carry_ref[...] = jnp.zeros_like(carry_ref)

    x = x_ref[...]
    h = _dot(x.astype(BF16), wqgm_ref[...])
    q = h[:, :QK_WIDTH] * cos_ref[...] + h[:, QK_WIDTH:2 * QK_WIDTH] * sin_ref[...]
    gate = h[:, 2 * QK_WIDTH:2 * QK_WIDTH + MIX_WIDTH]
    mq = h[:, 2 * QK_WIDTH + MIX_WIDTH:]

    qk_head = [_col_range_mask(QK_WIDTH, hh * RET_QK_DIM, (hh + 1) * RET_QK_DIM) for hh in range(RET_HEADS)]
    v_head = [_col_range_mask(MIX_WIDTH, hh * RET_V_DIM, (hh + 1) * RET_V_DIM) for hh in range(RET_HEADS)]
    outs = []
    for c in range(TOKEN_TILE // CHUNK):
        rows = slice(c * CHUNK, (c + 1) * CHUNK)
        qc = q[rows]
        kc = k_ref[rows, :]
        vc = v_ref[rows, :]
        kstack = jnp.concatenate([jnp.where(m, kc, jnp.zeros_like(kc)) for m in qk_head], axis=0)
        scores = _dot_nt(qc.astype(BF16), kstack) * mcat_ref[...]
        vstack = jnp.concatenate([jnp.where(m, vc, jnp.zeros_like(vc)) for m in v_head], axis=0)
        o = _dot(scores.astype(BF16), vstack)
        o = o + _dot((qc * dqf_ref[...]).astype(BF16), state_ref[...].astype(BF16))
        o = o + _dot((qc * dqb_ref[...]).astype(BF16), sb_ref[c])
        kf = (kc.astype(F32) * dkf_ref[...]).astype(BF16)
        state_ref[...] = state_ref[...] * rowdec_ref[...] + _dot_tn(kf, vc) * bdmask_ref[...]
        outs.append(o)
    o = jnp.concatenate(outs, axis=0)

    inv_dv = 1.0 / RET_V_DIM
    mu = jnp.zeros_like(o)
    for m in v_head:
        mu = jnp.where(m, jnp.sum(jnp.where(m, o, 0.0), axis=-1, keepdims=True) * inv_dv, mu)
    oc = o - mu
    var = jnp.zeros_like(o)
    for m in v_head:
        var = jnp.where(m, jnp.sum(jnp.where(m, oc * oc, 0.0), axis=-1, keepdims=True) * inv_dv, var)
    on = oc * lax.rsqrt(var + LN_EPS) * gng_ref[...] + gnb_ref[...]
    mix = _silu(gate) * on

    att = _memory_attention(mq, kbd_ref.at[0], vbd_ref.at[0])
    x1 = _out_proj_norm(x, mix, att, wout_ref, lng_ref, lnb_ref)
    o_ref[...] = x1

    x_hi = x1.astype(BF16)
    x_lo = (x1 - x_hi.astype(F32)).astype(BF16)
    logits = (_dot_nt(wrh_ref[...], x_hi) + _dot_nt(wrl_ref[...], x_hi) + _dot_nt(wrh_ref[...], x_lo)
              + br_ref[...])
    eid = lax.broadcasted_iota(jnp.int32, logits.shape, 0)
    logits = jnp.where(eid < N_EXPERTS, logits, -jnp.inf)
    m1 = jnp.max(logits, axis=0, keepdims=True)
    i1 = jnp.min(jnp.where(logits == m1, eid, ROUTER_ROWS), axis=0, keepdims=True)
    rest = jnp.where(eid == i1, -jnp.inf, logits)
    m2 = jnp.max(rest, axis=0, keepdims=True)
    i2 = jnp.min(jnp.where(rest == m2, eid, ROUTER_ROWS), axis=0, keepdims=True)
    e2 = jnp.exp(m2 - m1)
    g1 = 1.0 / (1.0 + e2)
    g2 = e2 / (1.0 + e2)
    sel = ((eid == i1) | (eid == i2)).astype(F32)
    prefix = _dot(sel.astype(BF16), tri_ref[...]) + carry_ref[:, 0:1]
    r1 = jnp.sum(jnp.where(eid == i1, prefix, 0.0), axis=0, keepdims=True)
    r2 = jnp.sum(jnp.where(eid == i2, prefix, 0.0), axis=0, keepdims=True)
    new_carry = carry_ref[...] + jnp.sum(sel, axis=1, keepdims=True)
    carry_ref[...] = new_carry
    cnt_ref[...] = new_carry
    row = lax.broadcasted_iota(jnp.int32, ridx_ref.shape, 0)
    ridx_ref[...] = jnp.where(row == 0, i1, jnp.where(row == 1, i2, jnp.where(
        row == 2, r1.astype(jnp.int32), jnp.where(row == 3, r2.astype(jnp.int32), 0))))
    rgate_ref[...] = jnp.where(row == 0, g1, jnp.where(row == 1, g2, 0.0))


def _ret_layer(x2d, batch, wqgm, cos_q, sin_q, kk, vv, sb, dqf, dkf, dqb, mcat, rowdec_f, bdmask,
               gng, gnb, kbd, vbd, wout, lng, lnb, wrh, wrl, br, tri):
    n = x2d.shape[0]
    nt = n // batch // TOKEN_TILE
    nc = TOKEN_TILE // CHUNK
    tok = lambda b, j: (b * nt + j, 0)
    const = lambda b, j: (0, 0)
    per_batch = lambda b, j: (b, 0, 0)
    lane_tok = lambda b, j: (0, b * nt + j)
    consts = [dqf, dkf, dqb, mcat, rowdec_f, bdmask, gng, gnb]
    tail = [wout, lng, lnb, wrh, wrl, br, tri]
    return pl.pallas_call(
        _ret_layer_kernel,
        grid=(batch, nt),
        in_specs=[
            pl.BlockSpec((TOKEN_TILE, D_MODEL), tok),
            pl.BlockSpec(wqgm.shape, const),
            pl.BlockSpec((TOKEN_TILE, QK_WIDTH), lambda b, j: (j, 0)),
            pl.BlockSpec((TOKEN_TILE, QK_WIDTH), lambda b, j: (j, 0)),
            pl.BlockSpec((TOKEN_TILE, QK_WIDTH), tok),
            pl.BlockSpec((TOKEN_TILE, MIX_WIDTH), tok),
            pl.BlockSpec((nc, QK_WIDTH, MIX_WIDTH), lambda b, j: (b * nt + j, 0, 0)),
        ] + [pl.BlockSpec(a.shape, const) for a in consts] + [
            pl.BlockSpec((1,) + kbd.shape[1:], per_batch),
            pl.BlockSpec((1,) + vbd.shape[1:], per_batch),
        ] + [pl.BlockSpec(a.shape, const) for a in tail],
        out_specs=[
            pl.BlockSpec((TOKEN_TILE, D_MODEL), tok),
            pl.BlockSpec((8, TOKEN_TILE), lane_tok),
            pl.BlockSpec((8, TOKEN_TILE), lane_tok),
            pl.BlockSpec((ROUTER_ROWS, 128), const),
        ],
        out_shape=[
            jax.ShapeDtypeStruct((n, D_MODEL), F32),
            jax.ShapeDtypeStruct((8, n), jnp.int32),
            jax.ShapeDtypeStruct((8, n), F32),
            jax.ShapeDtypeStruct((ROUTER_ROWS, 128), F32),
        ],
        scratch_shapes=[pltpu.VMEM((QK_WIDTH, MIX_WIDTH), F32), pltpu.VMEM((ROUTER_ROWS, 128), F32)],
        compiler_params=_params(2),
        name="ret_layer",
    )(x2d, wqgm, cos_q, sin_q, kk, vv, sb, *consts, kbd, vbd, *tail)


def _moe_kernel(blk_exp_ref, src_ref, src_next_ref, dst_ref, x_hbm, gate_ref, wg_ref, wu_ref, wd_ref,
                out_hbm, xg_ref, xb_ref, acc_ref, stage_ref, gsem, ssem):
    i = pl.program_id(0)
    j = pl.program_id(1)
    nb = pl.num_programs(0)
    nj = pl.num_programs(1)
    slot = i % 2

    def row_gather(idx_ref, r, s):
        return pltpu.make_async_copy(x_hbm.at[pl.ds(idx_ref[0, 0, r], 1)], xg_ref.at[s, pl.ds(r, 1)], gsem.at[s])

    def row_scatter(r, s):
        return pltpu.make_async_copy(stage_ref.at[s, pl.ds(r, 1)], out_hbm.at[pl.ds(dst_ref[0, 0, r], 1)], ssem.at[s])

    def start_gather(idx_ref, s):
        def body(r, carry):
            row_gather(idx_ref, r, s).start()
            return carry
        lax.fori_loop(0, MOE_ROW_TILE, body, 0, unroll=8)

    def wait_rows(make, s):
        def body(r, carry):
            make(r, s).wait()
            return carry
        lax.fori_loop(0, MOE_ROW_TILE, body, 0, unroll=8)

    @pl.when((i == 0) & (j == 0))
    def _():
        start_gather(src_ref, 0)

    @pl.when(j == 0)
    def _():
        wait_rows(lambda r, s: row_gather(src_ref, r, s), slot)
        xb_ref[...] = xg_ref[slot].astype(BF16)
        acc_ref[...] = jnp.zeros_like(acc_ref)

    @pl.when((j == 1) & (i + 1 < nb))
    def _():
        start_gather(src_next_ref, 1 - slot)

    xb = xb_ref[...]
    hh = _silu(_dot(xb, wg_ref[0])) * _dot(xb, wu_ref[0])
    acc_ref[...] += _dot(hh.astype(BF16), wd_ref[0])

    @pl.when(j == nj - 1)
    def _():
        @pl.when(i >= 2)
        def _():
            wait_rows(row_scatter, slot)

        stage_ref[slot] = acc_ref[...] * gate_ref[...]

        def body(r, carry):
            row_scatter(r, slot).start()
            return carry
        lax.fori_loop(0, MOE_ROW_TILE, body, 0, unroll=8)

        @pl.when(i == nb - 1)
        def _():
            wait_rows(row_scatter, slot)

            @pl.when(nb >= 2)
            def _():
                wait_rows(row_scatter, 1 - slot)


def _moe(x2d, blk_exp, src, dst, gate_sorted, wg, wu, wd, n_out_rows):
    n_blocks = src.shape[0]
    d_ff = wg.shape[2]
    nxt = lambda i, j, be: (jnp.minimum(i + 1, n_blocks - 1), 0, 0)
    cur = lambda i, j, be: (i, 0, 0)
    smem_idx = functools.partial(pl.BlockSpec, (1, 1, MOE_ROW_TILE), memory_space=pltpu.SMEM)
    grid_spec = pltpu.PrefetchScalarGridSpec(
        num_scalar_prefetch=1,
        grid=(n_blocks, d_ff // MOE_COL_TILE),
        in_specs=[
            smem_idx(cur),
            smem_idx(nxt),
            smem_idx(cur),
            pl.BlockSpec(memory_space=pl.ANY),
            pl.BlockSpec((MOE_ROW_TILE, 1), lambda i, j, be: (i, 0)),
            pl.BlockSpec((1, D_MODEL, MOE_COL_TILE), lambda i, j, be: (be[i], 0, j)),
            pl.BlockSpec((1, D_MODEL, MOE_COL_TILE), lambda i, j, be: (be[i], 0, j)),
            pl.BlockSpec((1, MOE_COL_TILE, D_MODEL), lambda i, j, be: (be[i], j, 0)),
        ],
        out_specs=pl.BlockSpec(memory_space=pl.ANY),
        scratch_shapes=[
            pltpu.VMEM((2, MOE_ROW_TILE, D_MODEL), F32),
            pltpu.VMEM((MOE_ROW_TILE, D_MODEL), BF16),
            pltpu.VMEM((MOE_ROW_TILE, D_MODEL), F32),
            pltpu.VMEM((2, MOE_ROW_TILE, D_MODEL), F32),
            pltpu.SemaphoreType.DMA((2,)),
            pltpu.SemaphoreType.DMA((2,)),
        ],
    )
    return pl.pallas_call(
        _moe_kernel,
        grid_spec=grid_spec,
        out_shape=jax.ShapeDtypeStruct((n_out_rows, D_MODEL), F32),
        compiler_params=_params(2),
        name="moe_ffn",
    )(blk_exp, src, src, dst, x2d, gate_sorted, wg, wu, wd)


def _combine_kernel(x_ref, y_ref, lng_ref, lnb_ref, o_ref):
    y = y_ref[:, :D_MODEL] + y_ref[:, D_MODEL:]
    o_ref[...] = _layer_norm(DEEPNORM_ALPHA * x_ref[...] + y, lng_ref[...], lnb_ref[...])


def _combine(x2d, y_pairs, lng, lnb):
    n = x2d.shape[0]
    return pl.pallas_call(
        _combine_kernel,
        grid=(n // TOKEN_TILE,),
        in_specs=[
            pl.BlockSpec((TOKEN_TILE, D_MODEL), lambda i: (i, 0)),
            pl.BlockSpec((TOKEN_TILE, TOP_K * D_MODEL), lambda i: (i, 0)),
            pl.BlockSpec(lng.shape, lambda i: (0, 0)),
            pl.BlockSpec(lnb.shape, lambda i: (0, 0)),
        ],
        out_specs=pl.BlockSpec((TOKEN_TILE, D_MODEL), lambda i: (i, 0)),
        out_shape=jax.ShapeDtypeStruct((n, D_MODEL), F32),
        compiler_params=_params(1),
        name="combine",
    )(x2d, y_pairs, lng, lnb)


def _rot_half_columns(w):
    k = w.shape[0]
    w4 = w.reshape(k, RET_HEADS, 2, RET_QK_DIM // 2)
    return jnp.concatenate([-w4[:, :, 1:2], w4[:, :, 0:1]], axis=2).reshape(k, QK_WIDTH)


def _rope_tables(seq):
    half = RET_QK_DIM // 2
    inv = ROPE_BASE ** (-jnp.arange(0, RET_QK_DIM, 2, dtype=F32) / RET_QK_DIM)
    ang = jnp.arange(seq, dtype=F32)[:, None] * inv[None, :]
    cos = jnp.tile(jnp.cos(ang), (1, 2 * RET_HEADS))
    sin = jnp.tile(jnp.sin(ang), (1, 2 * RET_HEADS))
    assert cos.shape == (seq, QK_WIDTH) and half * 2 == RET_QK_DIM
    return cos, sin


def _decay_tables(log_gf, log_gb):
    pos = jnp.arange(CHUNK, dtype=F32)
    per_col = lambda lg: jnp.repeat(lg, RET_QK_DIM)[None, :]
    dqf = jnp.exp(pos[:, None] * per_col(log_gf))
    dkf = jnp.exp((CHUNK - pos)[:, None] * per_col(log_gf))
    dqb = jnp.exp((CHUNK - 1 - pos)[:, None] * per_col(log_gb))
    dkb = jnp.exp((pos + 1)[:, None] * per_col(log_gb))
    diff = pos[:, None] - pos[None, :]
    expo = jnp.where(diff[None] >= 0, diff[None] * log_gf[:, None, None], -diff[None] * log_gb[:, None, None])
    mcat = jnp.exp(expo).transpose(1, 0, 2).reshape(CHUNK, RET_HEADS * CHUNK)
    rowdec_f = jnp.exp(CHUNK * jnp.repeat(log_gf, RET_QK_DIM))[:, None]
    rowdec_b = jnp.exp(CHUNK * jnp.repeat(log_gb, RET_QK_DIM))[:, None]
    return dqf, dkf, dqb, dkb, mcat, rowdec_f, rowdec_b


def kernel(x, mem, w_mem_kv, w_in_a, sg_ln_g, sg_ln_b, sg_w, sg_b, w_in_b, decay_logit_f, decay_logit_b, ret_gn_g, ret_gn_b, w_out, ln_g, ln_b, w_ff_gate, w_ff_up, w_ff_down, w_router, b_router, w_e_gate, w_e_up, w_e_down):
    batch, seq, d = x.shape
    n_tok = batch * seq
    assert d == D_MODEL and seq % TOKEN_TILE == 0 and TOKEN_TILE % CHUNK == 0
    assert w_in_a.shape[0] == 1 and w_in_b.shape[0] == 1 and w_out.shape[0] == DEPTH
    tiles_per_batch = seq // TOKEN_TILE
    row = lambda a: a.reshape(1, -1)

    x2d = x.reshape(n_tok, d)
    kbd, vbd = _mem_kv(mem, w_mem_kv)

    wcat = sg_w[0].transpose(1, 0, 2).reshape(CHUNK, SG_GROUPS * CHUNK).astype(BF16)
    sgbias = jnp.repeat(sg_b[0].T, SG_GROUP_DIM, axis=1)
    x2d = _sg_layer(x2d, tiles_per_batch, w_in_a[0].astype(BF16), row(sg_ln_g[0]), row(sg_ln_b[0]), wcat, sgbias,
                    kbd, vbd, w_out[0].astype(BF16), row(ln_g[0, 0]), row(ln_b[0, 0]))
    x2d = _dense_ffn(x2d, w_ff_gate[0].astype(BF16), w_ff_up[0].astype(BF16), w_ff_down[0].astype(BF16),
                     row(ln_g[0, 1]), row(ln_b[0, 1]))

    wb = w_in_b[0]
    wq, wk = wb[:, :QK_WIDTH], wb[:, QK_WIDTH:2 * QK_WIDTH]
    wv = wb[:, 2 * QK_WIDTH:2 * QK_WIDTH + MIX_WIDTH]
    wgate = wb[:, 2 * QK_WIDTH + MIX_WIDTH:2 * QK_WIDTH + 2 * MIX_WIDTH]
    wmq = wb[:, 2 * QK_WIDTH + 2 * MIX_WIDTH:]
    wkv = jnp.concatenate([wk, _rot_half_columns(wk), wv], axis=1).astype(BF16)
    wqgm = jnp.concatenate([wq, _rot_half_columns(wq), wgate, wmq], axis=1).astype(BF16)
    cos, sin = _rope_tables(seq)
    q_scale = RET_QK_DIM ** -0.5
    log_gf = jax.nn.log_sigmoid(decay_logit_f[0].astype(F32))
    log_gb = jax.nn.log_sigmoid(decay_logit_b[0].astype(F32))
    dqf, dkf, dqb, dkb, mcat, rowdec_f, rowdec_b = _decay_tables(log_gf, log_gb)
    bdmask = (jnp.arange(QK_WIDTH)[:, None] // RET_QK_DIM == jnp.arange(MIX_WIDTH)[None, :] // RET_V_DIM).astype(F32)

    kk, vv, sb = _ret_bwd(x2d, batch, wkv, cos, sin, dkb, rowdec_b, bdmask)

    wr_t = jnp.zeros((ROUTER_ROWS, d), F32).at[:N_EXPERTS].set(w_router[0].T)
    wr_hi = wr_t.astype(BF16)
    wr_lo = (wr_t - wr_hi.astype(F32)).astype(BF16)
    br = jnp.zeros((ROUTER_ROWS, 1), F32).at[:N_EXPERTS, 0].set(b_router[0].astype(F32))
    tri = (jnp.arange(TOKEN_TILE)[:, None] < jnp.arange(TOKEN_TILE)[None, :]).astype(BF16)
    x2d, ridx, rgate, counts = _ret_layer(
        x2d, batch, wqgm, cos * q_scale, sin * q_scale, kk, vv, sb, dqf, dkf, dqb, mcat, rowdec_f, bdmask,
        row(ret_gn_g[0]), row(ret_gn_b[0]), kbd, vbd, w_out[1].astype(BF16), row(ln_g[1, 0]), row(ln_b[1, 0]),
        wr_hi, wr_lo, br, tri)

    n_asg = n_tok * TOP_K
    n_rows = n_asg + N_EXPERTS * MOE_ROW_TILE
    n_blocks = n_rows // MOE_ROW_TILE
    cnt = counts[:N_EXPERTS, 0].astype(jnp.int32)
    padded = (cnt + MOE_ROW_TILE - 1) // MOE_ROW_TILE * MOE_ROW_TILE
    pad_end = jnp.cumsum(padded)
    pad_start = pad_end - padded
    exp_id = ridx[0:2].T.reshape(-1)
    dest = pad_start[exp_id] + ridx[2:4].T.reshape(-1)
    asg_of_row = jnp.full((n_rows,), -1, jnp.int32).at[dest].set(jnp.arange(n_asg, dtype=jnp.int32), unique_indices=True)
    is_pad = asg_of_row < 0
    src = jnp.where(is_pad, 0, asg_of_row // TOP_K)
    dst = jnp.where(is_pad, n_asg + jnp.cumsum(is_pad.astype(jnp.int32)) - 1, asg_of_row)
    gate_sorted = jnp.where(is_pad, 0.0, rgate[0:2].T.reshape(-1)[jnp.maximum(asg_of_row, 0)])
    blk_exp = jnp.minimum(jnp.searchsorted(pad_end, jnp.arange(n_blocks, dtype=jnp.int32) * MOE_ROW_TILE, side='right'),
                          N_EXPERTS - 1).astype(jnp.int32)
    as_blocks = lambda a: a.astype(jnp.int32).reshape(n_blocks, 1, MOE_ROW_TILE)
    y_rows = _moe(x2d, blk_exp, as_blocks(src), as_blocks(dst), gate_sorted.reshape(n_rows, 1),
                  w_e_gate[0].astype(BF16), w_e_up[0].astype(BF16), w_e_down[0].astype(BF16), n_rows)
    y_pairs = y_rows.reshape(n_rows // TOP_K, TOP_K * d)
    out = _combine(x2d, y_pairs, row(ln_g[1, 1]), row(ln_b[1, 1]))
    return out.reshape(batch, seq, d)
```
